```python
import math, functools
import jax, jax.numpy as jnp
from jax import lax
import numpy as np

D_MODEL = 2048
BATCH = 1
SEQ = 8192
DEPTH = 1
DEC_BATCH = 32
DEC_SEQ = 4
PAST_LEN = 8192
PAGE_SIZE = 128

N_META = 16
HEAD_DIM = 128
N_HEADS = D_MODEL // (2 * HEAD_DIM)
V_DIM = 2 * HEAD_DIM
QK_W = N_HEADS * 2 * HEAD_DIM
ATTN_W = N_HEADS * V_DIM
ROT_DIM = HEAD_DIM // 4
ROPE_THETA = 500000.0
ATTN_SCALE = HEAD_DIM ** -0.5
Q_BLOCK = 128
D_INNER = 2 * D_MODEL
SSM_HEAD_DIM = 64
N_SSM_HEADS = D_INNER // SSM_HEAD_DIM
D_STATE = 128
N_GROUPS = 8
HEADS_PER_GROUP = N_SSM_HEADS // N_GROUPS
CONV_W = 4
CONV_DIM = D_INNER + 2 * N_GROUPS * D_STATE
SSD_CHUNK = 128
D_FF = -(-8 * D_MODEL // (3 * 256)) * 256
EPS = 1e-6
IN_SIZES = [QK_W, QK_W, ATTN_W, D_INNER, CONV_DIM, N_SSM_HEADS, D_MODEL, D_MODEL]
IN_W = sum(IN_SIZES)
IN_SPLITS = [int(s) for s in np.cumsum(IN_SIZES[:-1])]

kernel_name = 'hybrid_diffattn_mamba2_gated_decoder_step'


def lambda_init(layer):
    return 0.8 - 0.6 * math.exp(-0.3 * layer)


def rms_norm(x, w):
    x32 = x.astype(jnp.float32)
    y = x32 * lax.rsqrt(jnp.mean(x32 * x32, axis=-1, keepdims=True) + EPS)
    return (y * w.astype(jnp.float32)).astype(x.dtype)


def group_rms_norm(x, w):
    shp = x.shape
    x32 = x.astype(jnp.float32).reshape(shp[:-1] + (N_GROUPS, shp[-1] // N_GROUPS))
    y = x32 * lax.rsqrt(jnp.mean(x32 * x32, axis=-1, keepdims=True) + EPS)
    return (y.reshape(shp) * w.astype(jnp.float32)).astype(x.dtype)


def rope(x, pos):
    half = ROT_DIM // 2
    inv = ROPE_THETA ** (-jnp.arange(half, dtype=jnp.float32) * 2.0 / ROT_DIM)
    ang = pos.astype(jnp.float32)[:, None] * inv[None, :]
    cos = jnp.cos(ang)[None, :, None, None, :]
    sin = jnp.sin(ang)[None, :, None, None, :]
    xr = x[..., :ROT_DIM].astype(jnp.float32)
    x1, x2 = xr[..., :half], xr[..., half:]
    rot = jnp.concatenate([x1 * cos - x2 * sin, x2 * cos + x1 * sin], axis=-1)
    return jnp.concatenate([rot.astype(x.dtype), x[..., ROT_DIM:]], axis=-1)


def diff_weights(s, lam):
    p = jax.nn.softmax(s, axis=-1)
    return p[:, :, 0] - lam * p[:, :, 1]


def diff_attn_prompt(q, k, v, lam):
    b, L = q.shape[:2]
    nblk = -(-L // Q_BLOCK)
    qb = jnp.pad(q, ((0, 0), (0, nblk * Q_BLOCK - L), (0, 0), (0, 0), (0, 0)))
    qb = jnp.moveaxis(qb.reshape(b, nblk, Q_BLOCK, N_HEADS, 2, HEAD_DIM), 1, 0)
    kpos = jnp.arange(L)

    def one_block(args):
        qi, blk = args
        qpos = blk * Q_BLOCK + jnp.arange(Q_BLOCK)
        s = jnp.einsum('bqhmd,bkhmd->bhmqk', qi, k, preferred_element_type=jnp.float32) * ATTN_SCALE
        s = jnp.where(kpos[None, :] <= qpos[:, None], s, -jnp.inf)
        w = diff_weights(s, lam).astype(v.dtype)
        return jnp.einsum('bhqk,bkhe->bqhe', w, v)

    out = lax.map(one_block, (qb, jnp.arange(nblk)))
    return jnp.moveaxis(out, 0, 1).reshape(b, nblk * Q_BLOCK, N_HEADS, V_DIM)[:, :L]


def diff_attn_sample(q, k, v, lam, k_past, v_past):
    B, T = q.shape[:2]
    kp = k_past.reshape(B, -1, N_HEADS, 2, HEAD_DIM)
    P = kp.shape[1]
    s_past = jnp.einsum('bqhmd,bkhmd->bhmqk', q, kp, preferred_element_type=jnp.float32) * ATTN_SCALE
    s_new = jnp.einsum('bqhmd,bkhmd->bhmqk', q, k, preferred_element_type=jnp.float32) * ATTN_SCALE
    s_new = jnp.where(jnp.tril(jnp.ones((T, T), bool)), s_new, -jnp.inf)
    w = diff_weights(jnp.concatenate([s_past, s_new], axis=-1), lam).astype(v.dtype)
    return (jnp.einsum('bhqk,bkhe->bqhe', w[..., :P], v_past.astype(v.dtype))
            + jnp.einsum('bhqk,bkhe->bqhe', w[..., P:], v))


def paged_gather(cache, page_table):
    pages = cache[page_table]
    B, NP = page_table.shape
    return pages.reshape((B, NP * PAGE_SIZE) + cache.shape[2:])


def causal_conv(xp, w, bias):
    L = xp.shape[1] - (CONV_W - 1)
    out = bias
    for i in range(CONV_W):
        out = out + xp[:, i:i + L] * w[i]
    return out


def ssd_scan(x, dt, a, bm, cm, state0, chunk, lead):
    b, L = x.shape[:2]
    f32 = jnp.float32
    tail = (-(L + lead)) % chunk
    nc = (lead + L + tail) // chunk

    def prep(t):
        t = jnp.pad(t.astype(f32), ((0, 0), (lead, tail)) + ((0, 0),) * (t.ndim - 2))
        return jnp.moveaxis(t.reshape((b, nc, chunk) + t.shape[2:]), 1, 0)

    causal = jnp.tril(jnp.ones((chunk, chunk), bool))

    def step(state, inp):
        xc, dtc, bc, cc = inp
        bh = jnp.repeat(bc, HEADS_PER_GROUP, axis=2)
        ch = jnp.repeat(cc, HEADS_PER_GROUP, axis=2)
        acum = jnp.cumsum(dtc * a, axis=1)
        seg = acum[:, :, None, :] - acum[:, None, :, :]
        decay = jnp.exp(jnp.where(causal[None, :, :, None], seg, -jnp.inf))
        w = jnp.einsum('bihn,bjhn->bijh', ch, bh) * decay * dtc[:, None]
        y = jnp.einsum('bijh,bjhp->bihp', w, xc)
        y = y + jnp.einsum('bihn,bhpn->bihp', ch, state) * jnp.exp(acum)[..., None]
        carry_w = jnp.exp(acum[:, -1:] - acum) * dtc
        state = (state * jnp.exp(acum[:, -1])[:, :, None, None]
                 + jnp.einsum('bjh,bjhn,bjhp->bhpn', carry_w, bh, xc))
        return state, y

    state, ys = lax.scan(step, state0.astype(f32), (prep(x), prep(dt), prep(bm), prep(cm)))
    y = jnp.moveaxis(ys, 0, 1).reshape(b, nc * chunk, N_SSM_HEADS, SSM_HEAD_DIM)[:, lead:lead + L]
    return y.astype(x.dtype), state.astype(state0.dtype)


def layer_forward(h, pos, conv_prev, ssm0, attend, chunk, lead, lam_init, lw):
    b, L, _ = h.shape
    f32 = jnp.float32
    hn = rms_norm(h, lw['norm_mix_w'])
    q, k, v, z, xbc, dt_raw, g_attn, g_ssm = jnp.split(hn @ lw['w_in'], IN_SPLITS, axis=-1)
    q = rope(q.reshape(b, L, N_HEADS, 2, HEAD_DIM), pos)
    k = rope(k.reshape(b, L, N_HEADS, 2, HEAD_DIM), pos)
    v = v.reshape(b, L, N_HEADS, V_DIM)
    lam = (jnp.exp(jnp.sum(lw['lambda_q1'].astype(f32) * lw['lambda_k1'].astype(f32)))
           - jnp.exp(jnp.sum(lw['lambda_q2'].astype(f32) * lw['lambda_k2'].astype(f32))) + lam_init)
    o = attend(q, k, v, lam)
    o = rms_norm(o, lw['subln_w']) * (1.0 - lam_init)
    attn_branch = o.reshape(b, L, ATTN_W) @ lw['w_attn_branch']
    xbc_all = jnp.concatenate([conv_prev.astype(h.dtype), xbc], axis=1)
    new_conv = xbc_all[:, -(CONV_W - 1):]
    xbc_c = jax.nn.silu(causal_conv(xbc_all, lw['conv_w'], lw['conv_b']))
    xs, bm, cm = jnp.split(xbc_c, [D_INNER, D_INNER + N_GROUPS * D_STATE], axis=-1)
    xs = xs.reshape(b, L, N_SSM_HEADS, SSM_HEAD_DIM)
    dt = jax.nn.softplus(dt_raw.astype(f32) + lw['dt_bias'].astype(f32))
    a = -jnp.exp(lw['a_log'].astype(f32))
    y, new_ssm = ssd_scan(xs, dt, a, bm.reshape(b, L, N_GROUPS, D_STATE),
                          cm.reshape(b, L, N_GROUPS, D_STATE), ssm0, chunk, lead)
    y = (y + xs * lw['d_skip'][:, None]).reshape(b, L, D_INNER) * jax.nn.silu(z)
    ssm_branch = group_rms_norm(y, lw['ssm_norm_w']) @ lw['w_ssm_branch']
    merged = jax.nn.sigmoid(g_attn) * attn_branch + jax.nn.sigmoid(g_ssm) * ssm_branch
    h = h + merged @ lw['w_out']
    hn = rms_norm(h, lw['norm_ffn_w'])
    h = h + (jax.nn.silu(hn @ lw['w_gate']) * (hn @ lw['w_up'])) @ lw['w_down']
    return h, k.reshape(b, L, N_HEADS, 2 * HEAD_DIM), v, new_ssm, new_conv


def setup_inputs(seed: int = 0) -> dict:
    key = jax.random.key(seed)
    ks = jax.random.split(key, 32)
    f32 = jnp.float32
    n_pages = PAST_LEN // PAGE_SIZE
    used = DEC_BATCH * n_pages
    n_phys = used + max(1, used // 4)

    def nrm(k, shape, scale):
        return jax.random.normal(k, shape, f32) * scale

    def gain(k, shape):
        return 1.0 + nrm(k, shape, 0.02)

    page_table = jax.random.permutation(ks[6], n_phys)[:used].reshape(DEC_BATCH, n_pages).astype(jnp.int32)
    dt0 = jnp.exp(jax.random.uniform(ks[16], (DEPTH, N_SSM_HEADS), f32, math.log(1e-3), math.log(1e-1)))
    dt_bias = dt0 + jnp.log(-jnp.expm1(-dt0))
    a_log = jnp.log(jax.random.uniform(ks[17], (DEPTH, N_SSM_HEADS), f32, 1.0, 16.0))
    return {
        'x_prompt': nrm(ks[0], (BATCH, SEQ, D_MODEL), 1.0),
        'x_sample': nrm(ks[1], (DEC_BATCH, DEC_SEQ, D_MODEL), 1.0),
        'cache_k': nrm(ks[2], (DEPTH, n_phys, PAGE_SIZE, N_HEADS, 2 * HEAD_DIM), 1.0),
        'cache_v': nrm(ks[3], (DEPTH, n_phys, PAGE_SIZE, N_HEADS, 2 * HEAD_DIM), 1.0),
        'state_ssm': nrm(ks[4], (DEPTH, DEC_BATCH, N_SSM_HEADS, SSM_HEAD_DIM, D_STATE), 0.1),
        'state_conv': nrm(ks[5], (DEPTH, DEC_BATCH, CONV_W - 1, CONV_DIM), 1.0),
        'page_table': page_table,
        'meta_tokens': nrm(ks[7], (N_META, D_MODEL), 1.0),
        'norm_mix_w': gain(ks[8], (DEPTH, D_MODEL)),
        'w_in': nrm(ks[9], (DEPTH, D_MODEL, IN_W), D_MODEL ** -0.5),
        'lambda_q1': nrm(ks[10], (DEPTH, HEAD_DIM), 0.1),
        'lambda_k1': nrm(ks[11], (DEPTH, HEAD_DIM), 0.1),
        'lambda_q2': nrm(ks[12], (DEPTH, HEAD_DIM), 0.1),
        'lambda_k2': nrm(ks[13], (DEPTH, HEAD_DIM), 0.1),
        'subln_w': gain(ks[14], (DEPTH, V_DIM)),
        'w_attn_branch': nrm(ks[15], (DEPTH, ATTN_W, D_MODEL), ATTN_W ** -0.5),
        'conv_w': nrm(ks[18], (DEPTH, CONV_W, CONV_DIM), CONV_W ** -0.5),
        'conv_b': nrm(ks[19], (DEPTH, CONV_DIM), 0.02),
        'dt_bias': dt_bias,
        'a_log': a_log,
        'd_skip': gain(ks[20], (DEPTH, N_SSM_HEADS)),
        'ssm_norm_w': gain(ks[21], (DEPTH, D_INNER)),
        'w_ssm_branch': nrm(ks[22], (DEPTH, D_INNER, D_MODEL), D_INNER ** -0.5),
        'w_out': nrm(ks[23], (DEPTH, D_MODEL, D_MODEL), D_MODEL ** -0.5),
        'norm_ffn_w': gain(ks[24], (DEPTH, D_MODEL)),
        'w_gate': nrm(ks[25], (DEPTH, D_MODEL, D_FF), D_MODEL ** -0.5),
        'w_up': nrm(ks[26], (DEPTH, D_MODEL, D_FF), D_MODEL ** -0.5),
        'w_down': nrm(ks[27], (DEPTH, D_FF, D_MODEL), D_FF ** -0.5),
        'final_norm_w': gain(ks[28], (D_MODEL,)),
    }


def reference(x_prompt, x_sample, cache_k, cache_v, state_ssm, state_conv, page_table, meta_tokens,
              norm_mix_w, w_in, lambda_q1, lambda_k1, lambda_q2, lambda_k2, subln_w, w_attn_branch,
              conv_w, conv_b, dt_bias, a_log, d_skip, ssm_norm_w, w_ssm_branch, w_out,
              norm_ffn_w, w_gate, w_up, w_down, final_norm_w):
    b, s, _ = x_prompt.shape
    t = x_sample.shape[1]
    past_len = page_table.shape[1] * PAGE_SIZE
    h_p = jnp.concatenate([jnp.broadcast_to(meta_tokens[None].astype(x_prompt.dtype), (b, N_META, D_MODEL)),
                           x_prompt], axis=1)
    h_s = x_sample
    pos_p = jnp.arange(N_META + s)
    pos_s = past_len + jnp.arange(t)
    lead = (-N_META) % SSD_CHUNK
    kp_l, vp_l, sp_l, cp_l, ks_l, vs_l, ss_l, cs_l = [], [], [], [], [], [], [], []
    for layer in range(DEPTH):
        lw = {
            'norm_mix_w': norm_mix_w[layer], 'w_in': w_in[layer],
            'lambda_q1': lambda_q1[layer], 'lambda_k1': lambda_k1[layer],
            'lambda_q2': lambda_q2[layer], 'lambda_k2': lambda_k2[layer],
            'subln_w': subln_w[layer], 'w_attn_branch': w_attn_branch[layer],
            'conv_w': conv_w[layer], 'conv_b': conv_b[layer], 'dt_bias': dt_bias[layer],
            'a_log': a_log[layer], 'd_skip': d_skip[layer], 'ssm_norm_w': ssm_norm_w[layer],
            'w_ssm_branch': w_ssm_branch[layer], 'w_out': w_out[layer], 'norm_ffn_w': norm_ffn_w[layer],
            'w_gate': w_gate[layer], 'w_up': w_up[layer], 'w_down': w_down[layer],
        }
        lam_init = lambda_init(layer)
        conv0 = jnp.zeros((b, CONV_W - 1, CONV_DIM), x_prompt.dtype)
        ssm0 = jnp.zeros((b, N_SSM_HEADS, SSM_HEAD_DIM, D_STATE), state_ssm.dtype)
        h_p, kp, vp, sp, cp = layer_forward(h_p, pos_p, conv0, ssm0, diff_attn_prompt,
                                            SSD_CHUNK, lead, lam_init, lw)
        attend_s = functools.partial(diff_attn_sample,
                                     k_past=paged_gather(cache_k[layer], page_table),
                                     v_past=paged_gather(cache_v[layer], page_table))
        h_s, ksn, vsn, ssn, csn = layer_forward(h_s, pos_s, state_conv[layer], state_ssm[layer], attend_s,
                                                t, 0, lam_init, lw)
        kp_l.append(kp); vp_l.append(vp); sp_l.append(sp); cp_l.append(cp)
        ks_l.append(ksn); vs_l.append(vsn); ss_l.append(ssn); cs_l.append(csn)
    y_prompt = rms_norm(h_p, final_norm_w)[:, N_META:]
    y_sample = rms_norm(h_s, final_norm_w)
    return (y_prompt, y_sample,
            jnp.stack(kp_l), jnp.stack(vp_l), jnp.stack(sp_l), jnp.stack(cp_l),
            jnp.stack(ks_l), jnp.stack(vs_l), jnp.stack(ss_l), jnp.stack(cs_l))
```

```python
import functools
import math

import jax
import jax.numpy as jnp
from jax import lax
from jax.experimental import pallas as pl
from jax.experimental.pallas import tpu as pltpu

D_MODEL = 2048
SEQ = 8192
DEC_BATCH = 32
DEC_SEQ = 4
PAGE_SIZE = 128
N_META = 16
HEAD_DIM = 128
N_HEADS = D_MODEL // (2 * HEAD_DIM)
V_DIM = 2 * HEAD_DIM
QK_W = N_HEADS * 2 * HEAD_DIM
ATTN_W = N_HEADS * V_DIM
ROT_DIM = HEAD_DIM // 4
ROT_HALF = ROT_DIM // 2
ROPE_THETA = 500000.0
ATTN_SCALE = HEAD_DIM ** -0.5
D_INNER = 2 * D_MODEL
SSM_HEAD_DIM = 64
N_SSM_HEADS = D_INNER // SSM_HEAD_DIM
D_STATE = 128
N_GROUPS = 8
HEADS_PER_GROUP = N_SSM_HEADS // N_GROUPS
GROUP_W = HEADS_PER_GROUP * SSM_HEAD_DIM
CONV_W = 4
CONV_DIM = D_INNER + 2 * N_GROUPS * D_STATE
SSD_CHUNK = 128
D_FF = -(-8 * D_MODEL // (3 * 256)) * 256
EPS = 1e-6
LAMBDA_INIT = 0.8 - 0.6 * math.exp(-0.3 * 0)

LANES = 128
SUBLANES = 8
LEAD = (-N_META) % SSD_CHUNK
ROWS_TOK0 = LEAD + N_META
ROWS_P = ROWS_TOK0 + SEQ
N_SAMPLE = DEC_BATCH * DEC_SEQ
ROWS = ROWS_P + N_SAMPLE
N_CHUNKS_P = ROWS_P // SSD_CHUNK
SROWS = DEC_BATCH * SUBLANES
NEG = -1e30

C_Q, C_K, C_V = 0, QK_W, 2 * QK_W
C_Z = C_V + ATTN_W
C_XBC = C_Z + D_INNER
C_GA = C_XBC + CONV_DIM
C_GS = C_GA + D_MODEL
C_DT = C_GS + D_MODEL
IN_W_PAD = C_DT + LANES

TM = 768
VMEM_LIMIT = 56 << 20

f32 = jnp.float32
bf16 = jnp.bfloat16
NT = (((1,), (1,)), ((), ()))
TN = (((0,), (0,)), ((), ()))


def _silu(x):
    return x * jax.nn.sigmoid(x)


def _softplus(x):
    return jnp.maximum(x, 0.0) + jnp.log1p(jnp.exp(-jnp.abs(x)))


def _params(sem, vmem=VMEM_LIMIT):
    return pltpu.CompilerParams(dimension_semantics=sem, vmem_limit_bytes=vmem)


def _rmsnorm_kernel(x_ref, w_ref, o_ref):
    x = x_ref[...]
    ms = jnp.mean(x * x, axis=-1, keepdims=True)
    o_ref[...] = (x * lax.rsqrt(ms + EPS) * w_ref[...]).astype(o_ref.dtype)


def _rmsnorm(x, w, out_dtype, tm, row_block0=0, n_rows=None, name="rmsnorm"):
    n_rows = x.shape[0] if n_rows is None else n_rows
    d = x.shape[1]
    return pl.pallas_call(
        _rmsnorm_kernel,
        grid=(n_rows // tm,),
        in_specs=[pl.BlockSpec((tm, d), lambda i: (i + row_block0, 0)),
                  pl.BlockSpec((1, d), lambda i: (0, 0))],
        out_specs=pl.BlockSpec((tm, d), lambda i: (i, 0)),
        out_shape=jax.ShapeDtypeStruct((n_rows, d), out_dtype),
        compiler_params=_params(("parallel",)),
        name=name,
    )(x, w.reshape(1, d))


def _matmul(name, a, w, col0, ncols, tn, epilogue, extra, out_specs, out_shape, tm=TM):
    m, k = a.shape
    jb0 = col0 // tn
    n_extra = len(extra)

    def kern(a_ref, w_ref, *refs):
        acc = jnp.dot(a_ref[...], w_ref[...], preferred_element_type=f32)
        epilogue(acc, refs[:n_extra], refs[n_extra:])

    in_specs = [pl.BlockSpec((tm, k), lambda i, j: (i, 0)),
                pl.BlockSpec((k, tn), lambda i, j: (0, j + jb0))]
    in_specs += [s for _, s in extra]
    return pl.pallas_call(
        kern,
        grid=(m // tm, ncols // tn),
        in_specs=in_specs,
        out_specs=out_specs,
        out_shape=out_shape,
        compiler_params=_params(("parallel", "parallel")),
        name=name,
    )(a, w, *[x for x, _ in extra])


def _tile_spec(tn, tm=TM):
    return pl.BlockSpec((tm, tn), lambda i, j: (i, j))


def _store_epilogue(acc, extra, outs):
    for o in outs:
        o[...] = acc.astype(o.dtype)


def _rope_epilogue(tn, acc, extra, outs):
    c_ref, lo_ref, hi_ref = extra
    reps = tn // LANES
    c = jnp.concatenate([c_ref[...]] * reps, axis=1)
    lo = jnp.concatenate([lo_ref[...]] * reps, axis=1)
    hi = jnp.concatenate([hi_ref[...]] * reps, axis=1)
    y = acc * c + pltpu.roll(acc, tn - ROT_HALF, 1) * lo + pltpu.roll(acc, ROT_HALF, 1) * hi
    for o in outs:
        o[...] = y.astype(o.dtype)


def _rope_tables(pos):
    inv = ROPE_THETA ** (-jnp.arange(ROT_HALF, dtype=f32) * 2.0 / ROT_DIM)
    ang = pos.astype(f32)[:, None] * inv[None, :]
    cos, sin = jnp.cos(ang), jnp.sin(ang)
    n = pos.shape[0]
    c = jnp.concatenate([cos, cos, jnp.ones((n, LANES - ROT_DIM), f32)], axis=1)
    lo = jnp.concatenate([-sin, jnp.zeros((n, LANES - ROT_HALF), f32)], axis=1)
    hi = jnp.concatenate([jnp.zeros((n, ROT_HALF), f32), sin, jnp.zeros((n, LANES - ROT_DIM), f32)], axis=1)
    return c, lo, hi


TQ = 768
TK = 768


def _lambda_value(lv_ref):
    lv = lv_ref[...]
    s1 = jnp.sum(lv[0:1] * lv[1:2], axis=-1, keepdims=True)
    s2 = jnp.sum(lv[2:3] * lv[3:4], axis=-1, keepdims=True)
    return jnp.exp(s1) - jnp.exp(s2) + LAMBDA_INIT


def _subln(o, subw):
    ms = jnp.mean(o * o, axis=-1, keepdims=True)
    return o * lax.rsqrt(ms + EPS) * subw * (1.0 - LAMBDA_INIT)


def _attn_prompt_kernel(q_ref, k_ref, v_ref, lv_ref, sw_ref, o_ref, m_ref, l_ref, acc_ref):
    qi = pl.program_id(1)
    ki = pl.program_id(2)
    nk = pl.num_programs(2)

    @pl.when(ki == 0)
    def _():
        m_ref[...] = jnp.full(m_ref.shape, NEG, f32)
        l_ref[...] = jnp.zeros(l_ref.shape, f32)
        acc_ref[...] = jnp.zeros(acc_ref.shape, f32)

    def step(masked):
        q = q_ref[...]
        k = k_ref[...]
        v = v_ref[...]
        if masked:
            row = qi * TQ + lax.broadcasted_iota(jnp.int32, (TQ, TK), 0)
            col = ki * TK + lax.broadcasted_iota(jnp.int32, (TQ, TK), 1)
            mask = (col <= row) & (col >= LEAD)
        for mp in range(2):
            sl = slice(mp * HEAD_DIM, (mp + 1) * HEAD_DIM)
            s = lax.dot_general(q[:, sl], k[:, sl], NT, preferred_element_type=f32) * ATTN_SCALE
            if masked:
                s = jnp.where(mask, s, NEG)
            m_old = m_ref[mp]
            m_new = jnp.maximum(m_old, jnp.max(s, axis=-1, keepdims=True))
            alpha = jnp.exp(m_old - m_new)
            p = jnp.exp(s - m_new)
            l_ref[mp] = alpha * l_ref[mp] + jnp.sum(p, axis=-1, keepdims=True)
            acc_ref[mp] = alpha * acc_ref[mp] + jnp.dot(p.astype(bf16), v, preferred_element_type=f32)
            m_ref[mp] = m_new

    @pl.when((ki < qi) & (ki > 0))
    def _():
        step(False)

    @pl.when(((ki == qi) | (ki == 0)) & (ki <= qi))
    def _():
        step(True)

    @pl.when(ki == nk - 1)
    def _():
        lam = _lambda_value(lv_ref)
        o = acc_ref[0] / l_ref[0] - lam * (acc_ref[1] / l_ref[1])
        o_ref[...] = _subln(o, sw_ref[...]).astype(o_ref.dtype)


def _attn_prompt(q, k, v, lam_vecs, subw):
    nq = ROWS // TQ
    return pl.pallas_call(
        _attn_prompt_kernel,
        grid=(N_HEADS, nq, nq),
        in_specs=[pl.BlockSpec((TQ, V_DIM), lambda h, i, j: (i, h)),
                  pl.BlockSpec((TK, V_DIM), lambda h, i, j: (jnp.minimum(i, j), h)),
                  pl.BlockSpec((TK, V_DIM), lambda h, i, j: (jnp.minimum(i, j), h)),
                  pl.BlockSpec((4, HEAD_DIM), lambda h, i, j: (0, 0)),
                  pl.BlockSpec((1, V_DIM), lambda h, i, j: (0, 0))],
        out_specs=pl.BlockSpec((TQ, V_DIM), lambda h, i, j: (i, h)),
        out_shape=jax.ShapeDtypeStruct((ROWS, ATTN_W), bf16),
        scratch_shapes=[pltpu.VMEM((2, TQ, 1), f32), pltpu.VMEM((2, TQ, 1), f32),
                        pltpu.VMEM((2, TQ, V_DIM), f32)],
        compiler_params=_params(("parallel", "parallel", "arbitrary")),
        name="attn_prompt",
    )(q, k, v, lam_vecs, subw)


QROWS = N_HEADS * 2 * DEC_SEQ
PAGE_ROWS = PAGE_SIZE * N_HEADS
PAGES_PER_STEP = 4


def _attn_sample_kernel(pt_ref, q_ref, kn_ref, vn_ref, *refs):
    kp_refs = refs[:PAGES_PER_STEP]
    vp_refs = refs[PAGES_PER_STEP:2 * PAGES_PER_STEP]
    lv_ref, sw_ref, o_ref, m_ref, l_ref, acc_ref = refs[2 * PAGES_PER_STEP:]
    p = pl.program_id(1)
    q = q_ref[...]

    def update(k2d, v2d, mask, first):
        s = lax.dot_general(q, k2d, NT, preferred_element_type=f32) * ATTN_SCALE
        s = jnp.where(mask, s, NEG)
        s_max = jnp.max(s, axis=-1, keepdims=True)
        if first:
            m_new = s_max
            pr = jnp.exp(s - m_new)
            l_ref[...] = jnp.sum(pr, axis=-1, keepdims=True)
            acc_ref[...] = jnp.dot(pr.astype(bf16), v2d, preferred_element_type=f32)
        else:
            m_old = m_ref[...]
            m_new = jnp.maximum(m_old, s_max)
            alpha = jnp.exp(m_old - m_new)
            pr = jnp.exp(s - m_new)
            l_ref[...] = alpha * l_ref[...] + jnp.sum(pr, axis=-1, keepdims=True)
            acc_ref[...] = alpha * acc_ref[...] + jnp.dot(pr.astype(bf16), v2d, preferred_element_type=f32)
        m_ref[...] = m_new

    @pl.when(p == 0)
    def _():
        n_new = DEC_SEQ * N_HEADS
        r = lax.broadcasted_iota(jnp.int32, (QROWS, n_new), 0)
        c = lax.broadcasted_iota(jnp.int32, (QROWS, n_new), 1)
        mask = (c % N_HEADS == r // (2 * DEC_SEQ)) & (c // N_HEADS <= r % DEC_SEQ)
        update(kn_ref[...], vn_ref[...], mask, True)

    r = lax.broadcasted_iota(jnp.int32, (QROWS, PAGE_ROWS), 0)
    c = lax.broadcasted_iota(jnp.int32, (QROWS, PAGE_ROWS), 1)
    same_head = c % N_HEADS == r // (2 * DEC_SEQ)
    for kp_ref, vp_ref in zip(kp_refs, vp_refs):
        update(kp_ref[...].astype(bf16), vp_ref[...].astype(bf16), same_head, False)

    @pl.when(p == pl.num_programs(1) - 1)
    def _():
        n = acc_ref[...] / l_ref[...]
        o = n - _lambda_value(lv_ref) * pltpu.roll(n, QROWS - DEC_SEQ, 0)
        o_ref[...] = _subln(o, sw_ref[...]).astype(o_ref.dtype)


def _attn_sample(page_table, q, k_new, v_new, cache_k, cache_v, lam_vecs, subw):
    n_pages = page_table.shape[1]
    n_new = DEC_SEQ * N_HEADS
    page_spec = lambda i: pl.BlockSpec((None, PAGE_ROWS, V_DIM),
                                       lambda b, p, pt: (pt[b, p * PAGES_PER_STEP + i], 0, 0))
    grid_spec = pltpu.PrefetchScalarGridSpec(
        num_scalar_prefetch=1,
        grid=(DEC_BATCH, n_pages // PAGES_PER_STEP),
        in_specs=[pl.BlockSpec((None, QROWS, V_DIM), lambda b, p, pt: (b, 0, 0)),
                  pl.BlockSpec((None, n_new, V_DIM), lambda b, p, pt: (b, 0, 0)),
                  pl.BlockSpec((None, n_new, V_DIM), lambda b, p, pt: (b, 0, 0))]
                 + [page_spec(i) for i in range(PAGES_PER_STEP)] * 2
                 + [pl.BlockSpec((4, HEAD_DIM), lambda b, p, pt: (0, 0)),
                    pl.BlockSpec((1, V_DIM), lambda b, p, pt: (0, 0))],
        out_specs=pl.BlockSpec((None, QROWS, V_DIM), lambda b, p, pt: (b, 0, 0)),
        scratch_shapes=[pltpu.VMEM((QROWS, 1), f32), pltpu.VMEM((QROWS, 1), f32),
                        pltpu.VMEM((QROWS, V_DIM), f32)],
    )
    return pl.pallas_call(
        _attn_sample_kernel,
        grid_spec=grid_spec,
        out_shape=jax.ShapeDtypeStruct((DEC_BATCH, QROWS, V_DIM), bf16),
        compiler_params=_params(("parallel", "arbitrary")),
        name="attn_sample",
    )(page_table, q, k_new, v_new, *([cache_k] * PAGES_PER_STEP), *([cache_v] * PAGES_PER_STEP), lam_vecs, subw)


COL_CHUNK = 1024


def _cumsum_rows(x, seg):
    n = x.shape[0]
    row = lax.broadcasted_iota(jnp.int32, x.shape, 0) % seg
    s = 1
    while s < seg:
        x = x + jnp.where(row >= s, pltpu.roll(x, s, 0), 0.0)
        s *= 2
    return x


def _ssd_scalars(dtr_ref, dtb_ref, alog_ref, valid, seg):
    dt = jnp.where(valid, _softplus(dtr_ref[...] + dtb_ref[...]), 0.0)
    a = -jnp.exp(alog_ref[...])
    acum = _cumsum_rows(dt * a, seg)
    return dt, acum


def _last_of_segment(x, seg):
    n = x.shape[0]
    row = lax.broadcasted_iota(jnp.int32, x.shape, 0)
    s = seg // 2
    while s >= 1:
        x = jnp.where(row % (2 * s) < s, pltpu.roll(x, n - s, 0), x)
        s //= 2
    return x


def _ssd_intra_chunk(xc_ref, r0, g, mask, dt, acum, acum_t, dt_t):
    rs = slice(r0, r0 + SSD_CHUNK)
    bg = xc_ref[rs, D_INNER + g * D_STATE:D_INNER + (g + 1) * D_STATE].astype(bf16)
    cg = xc_ref[rs, D_INNER + (N_GROUPS + g) * D_STATE:D_INNER + (N_GROUPS + g + 1) * D_STATE].astype(bf16)
    gmat = lax.dot_general(cg, bg, NT, preferred_element_type=f32)
    outs = []
    for hh in range(HEADS_PER_GROUP):
        h = g * HEADS_PER_GROUP + hh
        seg = acum[:, h:h + 1] - acum_t[h:h + 1, :]
        decay = jnp.exp(jnp.where(mask, seg, NEG))
        w = (gmat * decay * dt_t[h:h + 1, :]).astype(bf16)
        xh = xc_ref[rs, h * SSM_HEAD_DIM:(h + 1) * SSM_HEAD_DIM].astype(bf16)
        outs.append(jnp.dot(w, xh, preferred_element_type=f32))
    return outs


def _conv_silu_cols(x_windows, cw_ref, cb_ref, cs):
    acc = cb_ref[:, cs]
    for i in range(CONV_W):
        acc = acc + x_windows[i] * cw_ref[i:i + 1, cs]
    return _silu(acc)


def _gate_norm_store(y_ref, yacc, xc, z, dsk_ref, nw_ref, g):
    cs = slice(g * GROUP_W, (g + 1) * GROUP_W)
    y = (yacc + xc * dsk_ref[:, cs]) * _silu(z)
    ms = jnp.mean(y * y, axis=-1, keepdims=True)
    y_ref[:, cs] = (y * lax.rsqrt(ms + EPS) * nw_ref[:, cs]).astype(y_ref.dtype)


def _ssd_prompt_kernel(xbc_ref, z_ref, dtr_ref, cw_ref, cb_ref, dtb_ref, alog_ref, dsk_ref, nw_ref,
                       y_ref, st_ref, xp_ref, xc_ref, yacc_ref, xw_ref):
    c = pl.program_id(0)

    @pl.when(c == 0)
    def _():
        st_ref[...] = jnp.zeros(st_ref.shape, f32)
        xp_ref[0:SUBLANES, :] = jnp.zeros((SUBLANES, CONV_DIM), f32)

    @pl.when(c == N_CHUNKS_P)
    def _():
        y_ref[...] = jnp.zeros(y_ref.shape, y_ref.dtype)

    @pl.when(c < N_CHUNKS_P)
    def _():
        xp_ref[SUBLANES:, :] = xbc_ref[...]
        for j in range(CONV_DIM // COL_CHUNK):
            cs = slice(j * COL_CHUNK, (j + 1) * COL_CHUNK)
            wins = [xp_ref[SUBLANES - (CONV_W - 1) + i:SUBLANES - (CONV_W - 1) + i + SSD_CHUNK, cs]
                    for i in range(CONV_W)]
            xc_ref[:, cs] = _conv_silu_cols(wins, cw_ref, cb_ref, cs)
        xp_ref[0:SUBLANES, :] = xbc_ref[SSD_CHUNK - SUBLANES:, :]

        ii = lax.broadcasted_iota(jnp.int32, (SSD_CHUNK, SSD_CHUNK), 0)
        jj = lax.broadcasted_iota(jnp.int32, (SSD_CHUNK, SSD_CHUNK), 1)
        valid = c * SSD_CHUNK + ii >= LEAD
        dt, acum = _ssd_scalars(dtr_ref, dtb_ref, alog_ref, valid, SSD_CHUNK)
        mask = jj <= ii
        acum_t = acum.T
        dt_t = dt.T
        eac = jnp.exp(acum)
        alast = acum[SSD_CHUNK - 1:SSD_CHUNK, :]
        carry_w = jnp.exp(alast - acum) * dt
        elast = jnp.exp(alast)

        for g in range(N_GROUPS):
            gs = slice(g * GROUP_W, (g + 1) * GROUP_W)
            intra = _ssd_intra_chunk(xc_ref, 0, g, mask, dt, acum, acum_t, dt_t)
            bg = xc_ref[:, D_INNER + g * D_STATE:D_INNER + (g + 1) * D_STATE].astype(bf16)
            cg = xc_ref[:, D_INNER + (N_GROUPS + g) * D_STATE:D_INNER + (N_GROUPS + g + 1) * D_STATE].astype(bf16)
            st = st_ref[gs, :]
            ys = lax.dot_general(cg, st.astype(bf16), NT, preferred_element_type=f32)
            for hh in range(HEADS_PER_GROUP):
                h = g * HEADS_PER_GROUP + hh
                hs = slice(h * SSM_HEAD_DIM, (h + 1) * SSM_HEAD_DIM)
                ls = slice(hh * SSM_HEAD_DIM, (hh + 1) * SSM_HEAD_DIM)
                yacc_ref[:, hs] = intra[hh] + ys[:, ls] * eac[:, h:h + 1]
                xw_ref[:, ls] = (xc_ref[:, hs] * carry_w[:, h:h + 1]).astype(bf16)
            new = lax.dot_general(xw_ref[...], bg, TN, preferred_element_type=f32)
            for hh in range(HEADS_PER_GROUP):
                h = g * HEADS_PER_GROUP + hh
                rs = slice(g * GROUP_W + hh * SSM_HEAD_DIM, g * GROUP_W + (hh + 1) * SSM_HEAD_DIM)
                ls = slice(hh * SSM_HEAD_DIM, (hh + 1) * SSM_HEAD_DIM)
                st_ref[rs, :] = st_ref[rs, :] * elast[:, h:h + 1] + new[ls, :]

        for g in range(N_GROUPS):
            gs = slice(g * GROUP_W, (g + 1) * GROUP_W)
            _gate_norm_store(y_ref, yacc_ref[:, gs], xc_ref[:, gs], z_ref[:, gs], dsk_ref, nw_ref, g)


def _ssd_prompt(xbc, z, dtr, conv_w, conv_b, dt_bias, a_log, dskip, norm_w):
    row = lambda c: (c, 0)
    const = lambda c: (0, 0)
    return pl.pallas_call(
        _ssd_prompt_kernel,
        grid=(N_CHUNKS_P + 1,),
        in_specs=[pl.BlockSpec((SSD_CHUNK, CONV_DIM), row),
                  pl.BlockSpec((SSD_CHUNK, D_INNER), row),
                  pl.BlockSpec((SSD_CHUNK, LANES), row),
                  pl.BlockSpec((CONV_W, CONV_DIM), const),
                  pl.BlockSpec((1, CONV_DIM), const),
                  pl.BlockSpec((1, LANES), const),
                  pl.BlockSpec((1, LANES), const),
                  pl.BlockSpec((1, D_INNER), const),
                  pl.BlockSpec((1, D_INNER), const)],
        out_specs=[pl.BlockSpec((SSD_CHUNK, D_INNER), row),
                   pl.BlockSpec((D_INNER, D_STATE), const)],
        out_shape=[jax.ShapeDtypeStruct((ROWS, D_INNER), bf16),
                   jax.ShapeDtypeStruct((D_INNER, D_STATE), f32)],
        scratch_shapes=[pltpu.VMEM((SUBLANES + SSD_CHUNK, CONV_DIM), f32),
                        pltpu.VMEM((SSD_CHUNK, CONV_DIM), f32),
                        pltpu.VMEM((SSD_CHUNK, D_INNER), f32),
                        pltpu.VMEM((SSD_CHUNK, GROUP_W), bf16)],
        compiler_params=_params(("arbitrary",)),
        name="ssd_prompt",
    )(xbc, z, dtr, conv_w, conv_b, dt_bias, a_log, dskip, norm_w)


def _ssd_sample_kernel(xp_ref, z_ref, dtr_ref, st_in_ref, cw_ref, cb_ref, dtb_ref, alog_ref, dsk_ref, nw_ref,
                       y_ref, st_out_ref, xc_ref, yint_ref, eac_ref, cwt_ref, elast_ref, ytot_ref, xw_ref):
    b = pl.program_id(0)

    @pl.when(b == 0)
    def _():
        for j in range(CONV_DIM // COL_CHUNK):
            cs = slice(j * COL_CHUNK, (j + 1) * COL_CHUNK)
            x = xp_ref[:, cs]
            wins = [pltpu.roll(x, SROWS - (1 + i), 0) for i in range(CONV_W)]
            xc_ref[:, cs] = _conv_silu_cols(wins, cw_ref, cb_ref, cs)

        row = lax.broadcasted_iota(jnp.int32, (SROWS, LANES), 0)
        valid = row % SUBLANES < DEC_SEQ
        dt, acum = _ssd_scalars(dtr_ref, dtb_ref, alog_ref, valid, SUBLANES)
        alast = _last_of_segment(acum, SUBLANES)
        eac_ref[...] = jnp.exp(acum)
        cwt_ref[...] = jnp.exp(alast - acum) * dt
        elast_ref[...] = jnp.exp(alast)

        ii = lax.broadcasted_iota(jnp.int32, (SSD_CHUNK, SSD_CHUNK), 0)
        jj = lax.broadcasted_iota(jnp.int32, (SSD_CHUNK, SSD_CHUNK), 1)
        mask = (jj <= ii) & (ii // SUBLANES == jj // SUBLANES)
        for blk in range(SROWS // SSD_CHUNK):
            r0 = blk * SSD_CHUNK
            rs = slice(r0, r0 + SSD_CHUNK)
            dt_b = dt[rs]
            acum_b = acum[rs]
            acum_t = acum_b.T
            dt_t = dt_b.T
            for g in range(N_GROUPS):
                intra = _ssd_intra_chunk(xc_ref, r0, g, mask, dt_b, acum_b, acum_t, dt_t)
                for hh in range(HEADS_PER_GROUP):
                    h = g * HEADS_PER_GROUP + hh
                    yint_ref[rs, h * SSM_HEAD_DIM:(h + 1) * SSM_HEAD_DIM] = intra[hh]

    r0 = pl.multiple_of(b * SUBLANES, SUBLANES)
    rows = pl.ds(r0, SUBLANES)
    eac = eac_ref[rows, :]
    carry_w = cwt_ref[rows, :]
    elast = elast_ref[rows, :][0:1, :]
    for g in range(N_GROUPS):
        gs = slice(g * GROUP_W, (g + 1) * GROUP_W)
        bg = xc_ref[rows, D_INNER + g * D_STATE:D_INNER + (g + 1) * D_STATE].astype(bf16)
        cg = xc_ref[rows, D_INNER + (N_GROUPS + g) * D_STATE:D_INNER + (N_GROUPS + g + 1) * D_STATE].astype(bf16)
        st = st_in_ref[gs, :]
        ys = lax.dot_general(cg, st.astype(bf16), NT, preferred_element_type=f32)
        for hh in range(HEADS_PER_GROUP):
            h = g * HEADS_PER_GROUP + hh
            hs = slice(h * SSM_HEAD_DIM, (h + 1) * SSM_HEAD_DIM)
            ls = slice(hh * SSM_HEAD_DIM, (hh + 1) * SSM_HEAD_DIM)
            ytot_ref[:, hs] = yint_ref[rows, hs] + ys[:, ls] * eac[:, h:h + 1]
            xw_ref[:, ls] = (xc_ref[rows, hs] * carry_w[:, h:h + 1]).astype(bf16)
        new = lax.dot_general(xw_ref[...], bg, TN, preferred_element_type=f32)
        for hh in range(HEADS_PER_GROUP):
            h = g * HEADS_PER_GROUP + hh
            rs = slice(g * GROUP_W + hh * SSM_HEAD_DIM, g * GROUP_W + (hh + 1) * SSM_HEAD_DIM)
            ls = slice(hh * SSM_HEAD_DIM, (hh + 1) * SSM_HEAD_DIM)
            st_out_ref[rs, :] = st_in_ref[rs, :] * elast[:, h:h + 1] + new[ls, :]

    for g in range(N_GROUPS):
        gs = slice(g * GROUP_W, (g + 1) * GROUP_W)
        _gate_norm_store(y_ref, ytot_ref[:, gs], xc_ref[rows, gs], z_ref[rows, gs], dsk_ref, nw_ref, g)


def _ssd_sample(xp, z, dtr, state, conv_w, conv_b, dt_bias, a_log, dskip, norm_w):
    const = lambda b: (0, 0)
    return pl.pallas_call(
        _ssd_sample_kernel,
        grid=(DEC_BATCH,),
        in_specs=[pl.BlockSpec((SROWS, CONV_DIM), const),
                  pl.BlockSpec((SROWS, D_INNER), const),
                  pl.BlockSpec((SROWS, LANES), const),
                  pl.BlockSpec((None, D_INNER, D_STATE), lambda b: (b, 0, 0)),
                  pl.BlockSpec((CONV_W, CONV_DIM), const),
                  pl.BlockSpec((1, CONV_DIM), const),
                  pl.BlockSpec((1, LANES), const),
                  pl.BlockSpec((1, LANES), const),
                  pl.BlockSpec((1, D_INNER), const),
                  pl.BlockSpec((1, D_INNER), const)],
        out_specs=[pl.BlockSpec((SUBLANES, D_INNER), lambda b: (b, 0)),
                   pl.BlockSpec((None, D_INNER, D_STATE), lambda b: (b, 0, 0))],
        out_shape=[jax.ShapeDtypeStruct((SROWS, D_INNER), bf16),
                   jax.ShapeDtypeStruct((DEC_BATCH, D_INNER, D_STATE), f32)],
        scratch_shapes=[pltpu.VMEM((SROWS, CONV_DIM), f32),
                        pltpu.VMEM((SROWS, D_INNER), f32),
                        pltpu.VMEM((SROWS, LANES), f32),
                        pltpu.VMEM((SROWS, LANES), f32),
                        pltpu.VMEM((SROWS, LANES), f32),
                        pltpu.VMEM((SUBLANES, D_INNER), f32),
                        pltpu.VMEM((SUBLANES, GROUP_W), bf16)],
        compiler_params=_params(("arbitrary",)),
        name="ssd_sample",
    )(xp, z, dtr, state, conv_w, conv_b, dt_bias, a_log, dskip, norm_w)


TN_FF = 512


def _ffn_up_kernel(a_ref, wg_ref, wu_ref, o_ref):
    a = a_ref[...]
    g = jnp.dot(a, wg_ref[...], preferred_element_type=f32)
    u = jnp.dot(a, wu_ref[...], preferred_element_type=f32)
    o_ref[...] = (_silu(g) * u).astype(o_ref.dtype)


def _ffn_up(a, wg, wu):
    return pl.pallas_call(
        _ffn_up_kernel,
        grid=(ROWS // TM, D_FF // TN_FF),
        in_specs=[pl.BlockSpec((TM, D_MODEL), lambda i, j: (i, 0)),
                  pl.BlockSpec((D_MODEL, TN_FF), lambda i, j: (0, j)),
                  pl.BlockSpec((D_MODEL, TN_FF), lambda i, j: (0, j))],
        out_specs=pl.BlockSpec((TM, TN_FF), lambda i, j: (i, j)),
        out_shape=jax.ShapeDtypeStruct((ROWS, D_FF), bf16),
        compiler_params=_params(("parallel", "parallel")),
        name="ffn_up",
    )(a, wg, wu)


def kernel(x_prompt, x_sample, cache_k, cache_v, state_ssm, state_conv, page_table, meta_tokens, norm_mix_w, w_in, lambda_q1, lambda_k1, lambda_q2, lambda_k2, subln_w, w_attn_branch, conv_w, conv_b, dt_bias, a_log, d_skip, ssm_norm_w, w_ssm_branch, w_out, norm_ffn_w, w_gate, w_up, w_down, final_norm_w):
    assert x_prompt.shape == (1, SEQ, D_MODEL) and x_sample.shape == (DEC_BATCH, DEC_SEQ, D_MODEL)
    assert w_in.shape[0] == 1 and page_table.shape[0] == DEC_BATCH
    n_pages = page_table.shape[1]
    past_len = n_pages * PAGE_SIZE

    h0 = jnp.concatenate([jnp.zeros((LEAD, D_MODEL), f32), meta_tokens.astype(f32), x_prompt[0],
                          x_sample.reshape(N_SAMPLE, D_MODEL)], axis=0)
    wi = w_in[0]
    o_q, o_k, o_v, o_z, o_xbc, o_dt, o_ga, o_gs = [int(s) for s in
        (0, QK_W, 2 * QK_W, 2 * QK_W + ATTN_W, 2 * QK_W + ATTN_W + D_INNER,
         2 * QK_W + ATTN_W + D_INNER + CONV_DIM, 2 * QK_W + ATTN_W + D_INNER + CONV_DIM + N_SSM_HEADS,
         2 * QK_W + ATTN_W + D_INNER + CONV_DIM + N_SSM_HEADS + D_MODEL)]
    w_in_p = jnp.concatenate([wi[:, o_q:o_dt], wi[:, o_ga:], wi[:, o_dt:o_ga],
                              jnp.zeros((D_MODEL, LANES - N_SSM_HEADS), f32)], axis=1).astype(bf16)
    w_ab = w_attn_branch[0].astype(bf16)
    w_sb = w_ssm_branch[0].astype(bf16)
    w_o = w_out[0].astype(bf16)
    w_g = w_gate[0].astype(bf16)
    w_u = w_up[0].astype(bf16)
    w_d = w_down[0].astype(bf16)

    pos = jnp.concatenate([jnp.zeros((LEAD,), jnp.int32), jnp.arange(N_META + SEQ, dtype=jnp.int32),
                           jnp.tile(past_len + jnp.arange(DEC_SEQ, dtype=jnp.int32), DEC_BATCH)])
    rope_c, rope_lo, rope_hi = _rope_tables(pos)
    rope_extra = [(t, pl.BlockSpec((TM, LANES), lambda i, j: (i, 0))) for t in (rope_c, rope_lo, rope_hi)]
    lam_vecs = jnp.stack([lambda_q1[0], lambda_k1[0], lambda_q2[0], lambda_k2[0]]).astype(f32)
    subw = subln_w[0].reshape(1, V_DIM).astype(f32)
    pad_heads = lambda v, fill: jnp.concatenate(
        [v.astype(f32), jnp.full((LANES - N_SSM_HEADS,), fill, f32)]).reshape(1, LANES)
    dtb = pad_heads(dt_bias[0], 0.0)
    alog = pad_heads(a_log[0], 0.0)
    dskip = jnp.repeat(d_skip[0].astype(f32), SSM_HEAD_DIM).reshape(1, D_INNER)
    nw = ssm_norm_w[0].reshape(1, D_INNER).astype(f32)
    cw = conv_w[0].astype(f32)
    cb = conv_b[0].reshape(1, CONV_DIM).astype(f32)

    hn = _rmsnorm(h0, norm_mix_w[0], bf16, TM, name="norm_mix")
    sds = lambda n, dt: jax.ShapeDtypeStruct((ROWS, n), dt)
    tn = 1024
    q_bf, = _matmul("proj_q", hn, w_in_p, C_Q, QK_W, tn, functools.partial(_rope_epilogue, tn), rope_extra,
                    [_tile_spec(tn)], [sds(QK_W, bf16)])
    k_f32, k_bf = _matmul("proj_k", hn, w_in_p, C_K, QK_W, tn, functools.partial(_rope_epilogue, tn), rope_extra,
                          [_tile_spec(tn)] * 2, [sds(QK_W, f32), sds(QK_W, bf16)])
    v_f32, v_bf = _matmul("proj_v", hn, w_in_p, C_V, ATTN_W, tn, _store_epilogue, [],
                          [_tile_spec(tn)] * 2, [sds(ATTN_W, f32), sds(ATTN_W, bf16)])
    z, = _matmul("proj_z", hn, w_in_p, C_Z, D_INNER, tn, _store_epilogue, [], [_tile_spec(tn)], [sds(D_INNER, f32)])
    xbc, = _matmul("proj_xbc", hn, w_in_p, C_XBC, CONV_DIM, tn, _store_epilogue, [],
                   [_tile_spec(tn)], [sds(CONV_DIM, f32)])
    g_attn, = _matmul("proj_ga", hn, w_in_p, C_GA, D_MODEL, tn, _store_epilogue, [],
                      [_tile_spec(tn)], [sds(D_MODEL, f32)])
    g_ssm, = _matmul("proj_gs", hn, w_in_p, C_GS, D_MODEL, tn, _store_epilogue, [],
                     [_tile_spec(tn)], [sds(D_MODEL, f32)])
    dtr, = _matmul("proj_dt", hn, w_in_p, C_DT, LANES, LANES, _store_epilogue, [],
                   [_tile_spec(LANES)], [sds(LANES, f32)])

    o_all = _attn_prompt(q_bf, k_bf, v_bf, lam_vecs, subw)
    q_s = q_bf[ROWS_P:].reshape(DEC_BATCH, DEC_SEQ, N_HEADS, 2, HEAD_DIM)
    zq = jnp.zeros_like(q_s[..., 0, :])
    qbd = jnp.stack([jnp.concatenate([q_s[..., 0, :], zq], axis=-1),
                     jnp.concatenate([zq, q_s[..., 1, :]], axis=-1)], axis=1)
    qbd = qbd.transpose(0, 3, 1, 2, 4).reshape(DEC_BATCH, QROWS, V_DIM)
    pad_t = lambda t: jnp.pad(t.reshape(DEC_BATCH, DEC_SEQ, -1), ((0, 0), (0, SUBLANES - DEC_SEQ), (0, 0)))
    o_s = _attn_sample(page_table, qbd, k_bf[ROWS_P:].reshape(DEC_BATCH, DEC_SEQ * N_HEADS, V_DIM),
                       v_bf[ROWS_P:].reshape(DEC_BATCH, DEC_SEQ * N_HEADS, V_DIM),
                       cache_k.reshape(-1, PAGE_ROWS, V_DIM), cache_v.reshape(-1, PAGE_ROWS, V_DIM),
                       lam_vecs, subw)
    o_s = o_s.reshape(DEC_BATCH, N_HEADS, 2, DEC_SEQ, V_DIM)[:, :, 0].transpose(0, 2, 1, 3)
    o_all = o_all.at[ROWS_P:].set(o_s.reshape(N_SAMPLE, ATTN_W))

    yn_all, ssm_p = _ssd_prompt(xbc, z, dtr, cw, cb, dtb, alog, dskip, nw)
    xbc_s = xbc[ROWS_P:].reshape(DEC_BATCH, DEC_SEQ, CONV_DIM)
    xp_s = jnp.concatenate([jnp.zeros((DEC_BATCH, SUBLANES - DEC_SEQ - (CONV_W - 1), CONV_DIM), f32),
                            state_conv[0].astype(f32), xbc_s], axis=1)
    yn_s, ssm_s = _ssd_sample(xp_s.reshape(SROWS, CONV_DIM), pad_t(z[ROWS_P:]).reshape(SROWS, D_INNER),
                              pad_t(dtr[ROWS_P:]).reshape(SROWS, LANES),
                              state_ssm[0].reshape(DEC_BATCH, D_INNER, D_STATE),
                              cw, cb, dtb, alog, dskip, nw)
    yn_all = yn_all.at[ROWS_P:].set(yn_s.reshape(DEC_BATCH, SUBLANES, D_INNER)[:, :DEC_SEQ].reshape(N_SAMPLE, D_INNER))

    def gate_a(acc, extra, outs):
        outs[0][...] = jax.nn.sigmoid(extra[0][...]) * acc

    ma, = _matmul("attn_branch", o_all, w_ab, 0, D_MODEL, tn, gate_a, [(g_attn, _tile_spec(tn))],
                  [_tile_spec(tn)], [sds(D_MODEL, f32)])

    def gate_s(acc, extra, outs):
        outs[0][...] = (extra[1][...] + jax.nn.sigmoid(extra[0][...]) * acc).astype(bf16)

    merged, = _matmul("ssm_branch", yn_all, w_sb, 0, D_MODEL, tn, gate_s,
                      [(g_ssm, _tile_spec(tn)), (ma, _tile_spec(tn))], [_tile_spec(tn)], [sds(D_MODEL, bf16)])

    def resid_norm(acc, extra, outs):
        h = extra[0][...] + acc
        outs[0][...] = h
        ms = jnp.mean(h * h, axis=-1, keepdims=True)
        outs[1][...] = (h * lax.rsqrt(ms + EPS) * extra[1][...]).astype(bf16)

    full, tmo = D_MODEL, TM // 2
    h1, hn2 = _matmul("out_proj", merged, w_o, 0, D_MODEL, full, resid_norm,
                      [(h0, _tile_spec(full, tmo)), (norm_ffn_w[0].reshape(1, D_MODEL).astype(f32),
                                                     pl.BlockSpec((1, D_MODEL), lambda i, j: (0, 0)))],
                      [_tile_spec(full, tmo)] * 2, [sds(D_MODEL, f32), sds(D_MODEL, bf16)], tm=tmo)

    act = _ffn_up(hn2, w_g, w_u)

    def resid(acc, extra, outs):
        outs[0][...] = extra[0][...] + acc

    tnd = 512
    h2, = _matmul("ffn_down", act, w_d, 0, D_MODEL, tnd, resid, [(h1, _tile_spec(tnd))],
                  [_tile_spec(tnd)], [sds(D_MODEL, f32)])
    y_p = _rmsnorm(h2, final_norm_w, f32, SSD_CHUNK, row_block0=ROWS_TOK0 // SSD_CHUNK, n_rows=SEQ, name="norm_final_p")
    y_s = _rmsnorm(h2, final_norm_w, f32, N_SAMPLE, row_block0=ROWS_P // N_SAMPLE, n_rows=N_SAMPLE, name="norm_final_s")

    kv_shape_p = (1, 1, N_META + SEQ, N_HEADS, V_DIM)
    kv_shape_s = (1, DEC_BATCH, DEC_SEQ, N_HEADS, V_DIM)
    return (y_p.reshape(1, SEQ, D_MODEL),
            y_s.reshape(DEC_BATCH, DEC_SEQ, D_MODEL),
            k_f32[LEAD:ROWS_P].reshape(kv_shape_p),
            v_f32[LEAD:ROWS_P].reshape(kv_shape_p),
            ssm_p.reshape(1, 1, N_SSM_HEADS, SSM_HEAD_DIM, D_STATE),
            xbc[ROWS_P - (CONV_W - 1):ROWS_P].reshape(1, 1, CONV_W - 1, CONV_DIM),
            k_f32[ROWS_P:].reshape(kv_shape_s),
            v_f32[ROWS_P:].reshape(kv_shape_s),
            ssm_s.reshape(1, DEC_BATCH, N_SSM_HEADS, SSM_HEAD_DIM, D_STATE),
            xp_s[:, SUBLANES - (CONV_W - 1):].reshape(1, DEC_BATCH, CONV_W - 1, CONV_DIM))
```
